```python
import math
import jax, jax.numpy as jnp
from jax import lax
import numpy as np

D_MODEL = 1024
BATCH = 8
SEQ = 2048
DEPTH = 2
DEC_BATCH = 8
DEC_SEQ = 8192
PAST_LEN = 128

N_HEADS = 8
N_KV_HEADS = 2
HEAD_DIM = 64
D_ATT = N_HEADS * HEAD_DIM
D_KV = N_KV_HEADS * HEAD_DIM
WINDOW = 128
BLOCK = 128
N_BUCKETS = 32
MAX_DISTANCE = 128
N_FGROUPS = 4
FGROUP_DIM = 128
D_FOURIER = N_FGROUPS * FGROUP_DIM
D_RNN = 1024
N_RG_BLOCKS = 16
RG_BLOCK_DIM = D_RNN // N_RG_BLOCKS
RG_C = 8.0
CONV_RNN_WIDTH = 4
CONV_RNN_LEFT = 2
D_FF = 2816
CONV_FFN_WIDTH = 3
CONV_FFN_LEFT = 1
N_BRANCHES = 3
Q_END = D_ATT
K_END = Q_END + D_KV
V_END = K_END + D_KV
F_END = V_END + D_FOURIER
RX_END = F_END + D_RNN
RY_END = RX_END + D_RNN
N_IN = RY_END + N_BRANCHES * D_MODEL
DN_ALPHA = (2 * DEPTH) ** 0.25
DN_BETA = (8 * DEPTH) ** -0.25
LN_EPS = 1e-5

kernel_name = "hybrid_bidir_encoder_gqa_fnet_rglru"


def layer_norm(x, g, b):
    xf = x.astype(jnp.float32)
    mu = jnp.mean(xf, axis=-1, keepdims=True)
    var = jnp.mean(jnp.square(xf - mu), axis=-1, keepdims=True)
    return ((xf - mu) * lax.rsqrt(var + LN_EPS) * g.astype(jnp.float32) + b.astype(jnp.float32)).astype(x.dtype)


def depthwise_conv(x, w, b, left):
    K = w.shape[0]
    S = x.shape[1]
    xp = jnp.pad(x, ((0, 0), (left, K - 1 - left), (0, 0)))
    out = b
    for k in range(K):
        out = out + w[k] * xp[:, k:k + S]
    return out


def t5_bucket(rel):
    half = N_BUCKETS // 2
    max_exact = half // 2
    ret = jnp.where(rel > 0, half, 0)
    n = jnp.abs(rel)
    nf = jnp.maximum(n, 1).astype(jnp.float32)
    large = max_exact + (jnp.log(nf / max_exact) / math.log(MAX_DISTANCE / max_exact)
                         * (half - max_exact)).astype(jnp.int32)
    large = jnp.minimum(large, half - 1)
    return ret + jnp.where(n < max_exact, n, large)


def windowed_attention(q, k, v, sink, rel_bias):
    B, S, _ = q.shape
    nb = S // BLOCK
    G = N_HEADS // N_KV_HEADS
    scale = HEAD_DIM ** -0.5
    qb = q.reshape(B, nb, BLOCK, N_KV_HEADS, G, HEAD_DIM).transpose(1, 0, 2, 3, 4, 5)

    def kv_blocks(t):
        t = t.reshape(B, S, N_KV_HEADS, HEAD_DIM)
        tp = jnp.pad(t, ((0, 0), (BLOCK, BLOCK), (0, 0), (0, 0)))
        blocks = jnp.concatenate(
            [tp[:, i * BLOCK:i * BLOCK + S].reshape(B, nb, BLOCK, N_KV_HEADS, HEAD_DIM) for i in range(3)],
            axis=2)
        return blocks.transpose(1, 0, 2, 3, 4)

    kb = kv_blocks(k)
    vb = kv_blocks(v)
    qi = jnp.arange(BLOCK)[:, None]
    kj = jnp.arange(3 * BLOCK)[None, :]
    rel = kj - BLOCK - qi
    band_mask = jnp.abs(rel) <= WINDOW
    band_bias = rel_bias.astype(jnp.float32)[t5_bucket(rel)]
    band_bias = band_bias.transpose(2, 0, 1).reshape(N_KV_HEADS, G, BLOCK, 3 * BLOCK)
    sink_f = sink.astype(jnp.float32).reshape(N_KV_HEADS, G)[None, :, :, None, None]

    def one_block(args):
        q_blk, k_blk, v_blk, blk = args
        s = jnp.einsum('bqkgd,bnkd->bkgqn', q_blk, k_blk).astype(jnp.float32) * scale + band_bias
        kpos = blk * BLOCK - BLOCK + jnp.arange(3 * BLOCK)
        valid = band_mask & ((kpos >= 0) & (kpos < S))[None, :]
        s = jnp.where(valid, s, -jnp.inf)
        m = jnp.maximum(jnp.max(s, axis=-1, keepdims=True), sink_f)
        p = jnp.exp(s - m)
        denom = jnp.sum(p, axis=-1, keepdims=True) + jnp.exp(sink_f - m)
        w = (p / denom).astype(v_blk.dtype)
        return jnp.einsum('bkgqn,bnkd->bqkgd', w, v_blk)

    out = lax.map(one_block, (qb, kb, vb, jnp.arange(nb)))
    return out.transpose(1, 0, 2, 3, 4, 5).reshape(B, S, D_ATT)


def fourier_mix(xf):
    B, S, _ = xf.shape
    xg = xf.reshape(B, S, N_FGROUPS, FGROUP_DIM).astype(jnp.float32)
    y = jnp.fft.fftn(xg, axes=(1, 3), norm="ortho").real
    return y.reshape(B, S, D_FOURIER).astype(xf.dtype)


def _lin_combine(left, right):
    a1, b1 = left
    a2, b2 = right
    return (a1 * a2, a2 * b1 + b2)


def rglru(x, w_r, b_r, w_i, b_i, lam, reverse):
    B, S, _ = x.shape
    xb = x.reshape(B, S, N_RG_BLOCKS, RG_BLOCK_DIM)
    r = jax.nn.sigmoid((jnp.einsum('bsnc,ncd->bsnd', xb, w_r).reshape(B, S, D_RNN) + b_r).astype(jnp.float32))
    i = jax.nn.sigmoid((jnp.einsum('bsnc,ncd->bsnd', xb, w_i).reshape(B, S, D_RNN) + b_i).astype(jnp.float32))
    log_a = -RG_C * r * jax.nn.softplus(-lam.astype(jnp.float32))
    a = jnp.exp(log_a)
    b = jnp.sqrt(-jnp.expm1(2.0 * log_a)) * i * x.astype(jnp.float32)
    _, h = lax.associative_scan(_lin_combine, (a, b), axis=1, reverse=reverse)
    return h.astype(x.dtype)


def encoder_layer(x, rel_bias, w_in, b_in, attn_sink, w_att_o, w_four_o, conv_rnn_w, conv_rnn_b,
                  w_rg_r, b_rg_r, w_rg_i, b_rg_i, rg_lambda, w_rnn_o, w_out, b_out, ln1_g, ln1_b,
                  w_ffn_up, conv_ffn_w, conv_ffn_b, w_ffn_down, ln2_g, ln2_b):
    B, S, _ = x.shape
    u = x @ w_in + b_in
    att = windowed_attention(u[..., :Q_END], u[..., Q_END:K_END], u[..., K_END:V_END], attn_sink, rel_bias)
    o_att = att @ w_att_o
    o_four = fourier_mix(u[..., V_END:F_END]) @ w_four_o
    xr = depthwise_conv(u[..., F_END:RX_END], conv_rnn_w, conv_rnn_b, CONV_RNN_LEFT)
    h = (rglru(xr, w_rg_r[0], b_rg_r[0], w_rg_i[0], b_rg_i[0], rg_lambda[0], False)
         + rglru(xr, w_rg_r[1], b_rg_r[1], w_rg_i[1], b_rg_i[1], rg_lambda[1], True))
    o_rnn = (h * jax.nn.gelu(u[..., RX_END:RY_END])) @ w_rnn_o
    gates = jax.nn.sigmoid(u[..., RY_END:]).reshape(B, S, N_BRANCHES, D_MODEL)
    mixed = gates[:, :, 0] * o_att + gates[:, :, 1] * o_four + gates[:, :, 2] * o_rnn
    x = layer_norm(DN_ALPHA * x + mixed @ w_out + b_out, ln1_g, ln1_b)
    hu = depthwise_conv(x @ w_ffn_up, conv_ffn_w, conv_ffn_b, CONV_FFN_LEFT)
    ffn = (jax.nn.gelu(hu[..., :D_FF]) * hu[..., D_FF:]) @ w_ffn_down
    x = layer_norm(DN_ALPHA * x + ffn, ln2_g, ln2_b)
    return x


def setup_inputs(seed: int = 0) -> dict:
    key = jax.random.key(seed)
    ks = jax.random.split(key, 28)
    L = DEPTH
    nrm = jax.random.normal
    a0 = jax.random.uniform(ks[14], (L, 2, D_RNN), minval=0.9, maxval=0.999)
    return {
        "x_prompt": nrm(ks[0], (BATCH, SEQ, D_MODEL), jnp.float32),
        "x_sample": nrm(ks[1], (DEC_BATCH, DEC_SEQ, D_MODEL), jnp.float32),
        "rel_bias": 0.2 * nrm(ks[2], (N_BUCKETS, N_HEADS), jnp.float32),
        "w_in": nrm(ks[3], (L, D_MODEL, N_IN), jnp.float32) * D_MODEL ** -0.5,
        "b_in": 0.02 * nrm(ks[4], (L, N_IN), jnp.float32),
        "attn_sink": 0.5 * nrm(ks[5], (L, N_HEADS), jnp.float32),
        "w_att_o": nrm(ks[6], (L, D_ATT, D_MODEL), jnp.float32) * D_ATT ** -0.5,
        "w_four_o": nrm(ks[7], (L, D_FOURIER, D_MODEL), jnp.float32) * D_FOURIER ** -0.5,
        "conv_rnn_w": nrm(ks[8], (L, CONV_RNN_WIDTH, D_RNN), jnp.float32) * CONV_RNN_WIDTH ** -0.5,
        "conv_rnn_b": 0.02 * nrm(ks[9], (L, D_RNN), jnp.float32),
        "w_rg_r": nrm(ks[10], (L, 2, N_RG_BLOCKS, RG_BLOCK_DIM, RG_BLOCK_DIM), jnp.float32) * RG_BLOCK_DIM ** -0.5,
        "b_rg_r": 0.02 * nrm(ks[11], (L, 2, D_RNN), jnp.float32),
        "w_rg_i": nrm(ks[12], (L, 2, N_RG_BLOCKS, RG_BLOCK_DIM, RG_BLOCK_DIM), jnp.float32) * RG_BLOCK_DIM ** -0.5,
        "b_rg_i": 0.02 * nrm(ks[13], (L, 2, D_RNN), jnp.float32),
        "rg_lambda": jnp.log(a0) - jnp.log1p(-a0),
        "w_rnn_o": nrm(ks[15], (L, D_RNN, D_MODEL), jnp.float32) * D_RNN ** -0.5,
        "w_out": nrm(ks[16], (L, D_MODEL, D_MODEL), jnp.float32) * (D_MODEL ** -0.5 * DN_BETA),
        "b_out": 0.02 * nrm(ks[17], (L, D_MODEL), jnp.float32),
        "ln1_g": 1.0 + 0.02 * nrm(ks[18], (L, D_MODEL), jnp.float32),
        "ln1_b": 0.02 * nrm(ks[19], (L, D_MODEL), jnp.float32),
        "w_ffn_up": nrm(ks[20], (L, D_MODEL, 2 * D_FF), jnp.float32) * D_MODEL ** -0.5,
        "conv_ffn_w": nrm(ks[21], (L, CONV_FFN_WIDTH, 2 * D_FF), jnp.float32) * CONV_FFN_WIDTH ** -0.5,
        "conv_ffn_b": 0.02 * nrm(ks[22], (L, 2 * D_FF), jnp.float32),
        "w_ffn_down": nrm(ks[23], (L, D_FF, D_MODEL), jnp.float32) * (D_FF ** -0.5 * DN_BETA),
        "ln2_g": 1.0 + 0.02 * nrm(ks[24], (L, D_MODEL), jnp.float32),
        "ln2_b": 0.02 * nrm(ks[25], (L, D_MODEL), jnp.float32),
    }


def reference(x_prompt, x_sample, rel_bias, w_in, b_in, attn_sink, w_att_o, w_four_o, conv_rnn_w,
              conv_rnn_b, w_rg_r, b_rg_r, w_rg_i, b_rg_i, rg_lambda, w_rnn_o, w_out, b_out, ln1_g, ln1_b,
              w_ffn_up, conv_ffn_w, conv_ffn_b, w_ffn_down, ln2_g, ln2_b):
    y_prompt = x_prompt
    y_sample = x_sample
    for l in range(DEPTH):
        layer_params = (w_in[l], b_in[l], attn_sink[l], w_att_o[l], w_four_o[l], conv_rnn_w[l], conv_rnn_b[l],
                        w_rg_r[l], b_rg_r[l], w_rg_i[l], b_rg_i[l], rg_lambda[l], w_rnn_o[l], w_out[l], b_out[l],
                        ln1_g[l], ln1_b[l], w_ffn_up[l], conv_ffn_w[l], conv_ffn_b[l], w_ffn_down[l],
                        ln2_g[l], ln2_b[l])
        y_prompt = encoder_layer(y_prompt, rel_bias, *layer_params)
        y_sample = encoder_layer(y_sample, rel_bias, *layer_params)
    return (y_prompt, y_sample)
```

```python
import functools
import math

import jax
import jax.numpy as jnp
import numpy as np
from jax import lax
from jax.experimental import pallas as pl
from jax.experimental.pallas import tpu as pltpu

D_MODEL = 1024
DEPTH = 2
N_HEADS = 8
N_KV_HEADS = 2
HEAD_DIM = 64
D_ATT = N_HEADS * HEAD_DIM
D_KV = N_KV_HEADS * HEAD_DIM
WINDOW = 128
BLOCK = 128
N_BUCKETS = 32
MAX_DISTANCE = 128
N_FGROUPS = 4
FGROUP_DIM = 128
D_FOURIER = N_FGROUPS * FGROUP_DIM
D_RNN = 1024
N_RG_BLOCKS = 16
RG_BLOCK_DIM = D_RNN // N_RG_BLOCKS
RG_C = 8.0
CONV_RNN_WIDTH = 4
CONV_RNN_LEFT = 2
D_FF = 2816
CONV_FFN_WIDTH = 3
N_BRANCHES = 3
Q_END = D_ATT
K_END = Q_END + D_KV
V_END = K_END + D_KV
F_END = V_END + D_FOURIER
RX_END = F_END + D_RNN
RY_END = RX_END + D_RNN
N_IN = RY_END + N_BRANCHES * D_MODEL
DN_ALPHA = (2 * DEPTH) ** 0.25
LN_EPS = 1e-5
ATT_SCALE = HEAD_DIM ** -0.5

LANES = 128
SUBLANES = 8
VMEM_LIMIT_BYTES = 56 * 1024 * 1024

TOKEN_TILE = 512
FFN_CHUNK = 256
RNN_TILE = 128
RNN_PITCH_PAD = SUBLANES
FFT_RADIX = 4
FFT_ROW_CHUNK = 512

_BF = jnp.bfloat16
_F32 = jnp.float32


def _cparams(sem):
    return pltpu.CompilerParams(dimension_semantics=sem, vmem_limit_bytes=VMEM_LIMIT_BYTES)


def _const_spec(shape):
    nd = len(shape)
    return pl.BlockSpec(shape, lambda *_: (0,) * nd, pipeline_mode=pl.Buffered(1))


def _dot(a, b):
    return jnp.dot(a, b, preferred_element_type=_F32)


def _gelu(x):
    c = math.sqrt(2.0 / math.pi)
    return 0.5 * x * (1.0 + jnp.tanh(c * (x + 0.044715 * (x * x * x))))


def _sigmoid(x):
    return 1.0 / (1.0 + jnp.exp(-x))


def _layer_norm(z, g, b):
    mu = jnp.mean(z, axis=-1, keepdims=True)
    zc = z - mu
    var = jnp.mean(zc * zc, axis=-1, keepdims=True)
    return zc * lax.rsqrt(var + LN_EPS) * g + b


_KV_DUP = 2 * D_KV
_P1_COLS = D_ATT + 2 * _KV_DUP + D_FOURIER + D_RNN


def _in_proj_kernel(x_ref, w_ref, b_ref, q_ref, k_ref, v_ref, f_ref, rx_ref):
    xb = x_ref[...].astype(_BF)
    c = 0
    q = _dot(xb, w_ref[:, c:c + D_ATT]) + b_ref[:, c:c + D_ATT]
    q_ref[...] = (q * ATT_SCALE).astype(_BF)
    c += D_ATT
    k_ref[...] = (_dot(xb, w_ref[:, c:c + _KV_DUP]) + b_ref[:, c:c + _KV_DUP]).astype(_BF)
    c += _KV_DUP
    v_ref[...] = (_dot(xb, w_ref[:, c:c + _KV_DUP]) + b_ref[:, c:c + _KV_DUP]).astype(_BF)
    c += _KV_DUP
    f_ref[...] = _dot(xb, w_ref[:, c:c + D_FOURIER]) + b_ref[:, c:c + D_FOURIER]
    c += D_FOURIER
    rx_ref[...] = _dot(xb, w_ref[:, c:c + D_RNN]) + b_ref[:, c:c + D_RNN]


def _in_proj(x2d, w1, b1):
    n = x2d.shape[0]
    t = TOKEN_TILE
    row = lambda width: pl.BlockSpec((t, width), lambda i: (i, 0))
    return pl.pallas_call(
        _in_proj_kernel,
        grid=(n // t,),
        in_specs=[row(D_MODEL), _const_spec(w1.shape), _const_spec(b1.shape)],
        out_specs=[row(D_ATT), row(_KV_DUP), row(_KV_DUP), row(D_FOURIER), row(D_RNN)],
        out_shape=[
            jax.ShapeDtypeStruct((n, D_ATT), _BF),
            jax.ShapeDtypeStruct((n, _KV_DUP), _BF),
            jax.ShapeDtypeStruct((n, _KV_DUP), _BF),
            jax.ShapeDtypeStruct((n, D_FOURIER), _F32),
            jax.ShapeDtypeStruct((n, D_RNN), _F32),
        ],
        compiler_params=_cparams(("parallel",)),
        name="in_proj",
    )(x2d, w1, b1)


def _t5_bucket_np():
    qi = np.arange(BLOCK)[:, None]
    kj = np.arange(3 * BLOCK)[None, :]
    rel = kj - BLOCK - qi
    half = N_BUCKETS // 2
    max_exact = half // 2
    ret = np.where(rel > 0, half, 0)
    n = np.abs(rel)
    nf = np.maximum(n, 1).astype(np.float32)
    large = max_exact + (np.log(nf / np.float32(max_exact)) / np.float32(math.log(MAX_DISTANCE / max_exact))
                         * (half - max_exact)).astype(np.int32)
    large = np.minimum(large, half - 1)
    bucket = ret + np.where(n < max_exact, n, large)
    bucket = np.where(np.abs(rel) <= WINDOW, bucket, -1)
    return bucket.astype(np.int32)


def _bias_table_kernel(rb_ref, bucket_ref, out_ref):
    bucket = bucket_ref[...]
    for h in range(N_HEADS):
        acc = jnp.full((BLOCK, 3 * BLOCK), -jnp.inf, _F32)
        for k in range(N_BUCKETS):
            acc = jnp.where(bucket == k, rb_ref[k, h], acc)
        out_ref[h] = acc


def _bias_table(rel_bias):
    bucket = jnp.asarray(_t5_bucket_np())
    return pl.pallas_call(
        _bias_table_kernel,
        in_specs=[pl.BlockSpec(memory_space=pltpu.SMEM), pl.BlockSpec(memory_space=pltpu.VMEM)],
        out_specs=pl.BlockSpec(memory_space=pltpu.VMEM),
        out_shape=jax.ShapeDtypeStruct((N_HEADS, BLOCK, 3 * BLOCK), _F32),
        name="bias_table",
    )(rel_bias.astype(_F32), bucket)


_GROUP = N_HEADS // N_KV_HEADS


def _attention_kernel(q_ref, kp_ref, kc_ref, kn_ref, vp_ref, vc_ref, vn_ref, bias_ref, sink_ref, o_ref):
    j = pl.program_id(1)
    nb = pl.num_programs(1)
    lane = lax.broadcasted_iota(jnp.int32, (BLOCK, LANES), 1)
    low = lane < HEAD_DIM
    q = q_ref[0]
    zero = jnp.zeros((), _BF)
    for g in range(N_KV_HEADS):
        parts = []
        for p in range(_GROUP // 2):
            slab = q[:, (2 * g + p) * LANES:(2 * g + p + 1) * LANES]
            parts.append(jnp.where(low, slab, zero))
            parts.append(jnp.where(low, zero, slab))
        qs = jnp.concatenate(parts, axis=0)
        kg = slice(g * LANES, (g + 1) * LANES)
        nt = (((1,), (1,)), ((), ()))
        s_p = lax.dot_general(qs, kp_ref[0][:, kg], nt, preferred_element_type=_F32)
        s_c = lax.dot_general(qs, kc_ref[0][:, kg], nt, preferred_element_type=_F32)
        s_n = lax.dot_general(qs, kn_ref[0][:, kg], nt, preferred_element_type=_F32)
        bias = bias_ref[g]
        neg = -jnp.inf
        s_p = jnp.where(j > 0, s_p + bias[:, 0:BLOCK], neg)
        s_c = s_c + bias[:, BLOCK:2 * BLOCK]
        s_n = jnp.where(j < nb - 1, s_n + bias[:, 2 * BLOCK:3 * BLOCK], neg)
        sink = sink_ref[g]
        m = jnp.maximum(jnp.maximum(s_p, s_c), s_n)
        m = jnp.maximum(jnp.max(m, axis=-1, keepdims=True), sink)
        p_p = jnp.exp(s_p - m)
        p_c = jnp.exp(s_c - m)
        p_n = jnp.exp(s_n - m)
        denom = jnp.sum(p_p + p_c + p_n, axis=-1, keepdims=True) + jnp.exp(sink - m)
        pv = (_dot(p_p.astype(_BF), vp_ref[0][:, kg]) + _dot(p_c.astype(_BF), vc_ref[0][:, kg])
              + _dot(p_n.astype(_BF), vn_ref[0][:, kg]))
        o = pv * (1.0 / denom)
        for p in range(_GROUP // 2):
            even = o[(2 * p) * BLOCK:(2 * p + 1) * BLOCK]
            odd = o[(2 * p + 1) * BLOCK:(2 * p + 2) * BLOCK]
            c0 = (2 * g + p) * LANES
            o_ref[0, :, c0:c0 + LANES] = jnp.where(low, even, odd).astype(_BF)


def _attention(q, k2, v2, bias_g, sink_g):
    b, s, _ = q.shape
    nb = s // BLOCK
    cur = lambda width: pl.BlockSpec((1, BLOCK, width), lambda bi, j: (bi, j, 0))
    prv = lambda width: pl.BlockSpec((1, BLOCK, width), lambda bi, j: (bi, jnp.maximum(j - 1, 0), 0))
    nxt = lambda width: pl.BlockSpec((1, BLOCK, width), lambda bi, j: (bi, jnp.minimum(j + 1, nb - 1), 0))
    return pl.pallas_call(
        _attention_kernel,
        grid=(b, nb),
        in_specs=[cur(D_ATT), prv(_KV_DUP), cur(_KV_DUP), nxt(_KV_DUP),
                  prv(_KV_DUP), cur(_KV_DUP), nxt(_KV_DUP),
                  _const_spec(bias_g.shape), _const_spec(sink_g.shape)],
        out_specs=cur(D_ATT),
        out_shape=jax.ShapeDtypeStruct((b, s, D_ATT), _BF),
        compiler_params=_cparams(("parallel", "parallel")),
        name="attention",
    )(q, k2, k2, k2, v2, v2, v2, bias_g, sink_g)


def _fft_tables(s):
    nd = s // FFT_RADIX
    k = np.arange(nd, dtype=np.float64)
    ang = 2.0 * np.pi * np.outer(k, k) / nd
    dft = np.concatenate([np.cos(ang), np.sin(ang)], axis=0)
    tw = np.zeros((FFT_RADIX - 1, 2, nd, LANES), np.float32)
    for s2 in range(1, FFT_RADIX):
        a = 2.0 * np.pi * s2 * k / s
        tw[s2 - 1, 0] = np.cos(a)[:, None]
        tw[s2 - 1, 1] = np.sin(a)[:, None]
    c = np.arange(FGROUP_DIM, dtype=np.float64)
    angc = 2.0 * np.pi * np.outer(c, c) / FGROUP_DIM
    norm = 1.0 / math.sqrt(s * FGROUP_DIM)
    chan = np.concatenate([np.cos(angc), np.sin(angc)], axis=0) * norm
    return (jnp.asarray(dft, _BF), jnp.asarray(tw), jnp.asarray(chan, _BF))


def _fourier_kernel(f_ref, dft_ref, tw_ref, chan_ref, y_ref, *, nd, rows):
    xs = [f_ref[0, pl.ds(s2, nd, stride=FFT_RADIX), :].astype(_BF) for s2 in range(FFT_RADIX)]
    x01 = jnp.concatenate(xs[0:2], axis=1)
    x23 = jnp.concatenate(xs[2:4], axis=1)
    chan = chan_ref[...]

    def body(c, carry):
        r0 = pl.multiple_of(c * rows, rows)
        cm = dft_ref[pl.ds(r0, rows), :]
        sm = dft_ref[pl.ds(nd + r0, rows), :]
        ur01, ui01 = _dot(cm, x01), _dot(sm, x01)
        ur23, ui23 = _dot(cm, x23), _dot(sm, x23)
        ur = [ur01[:, :LANES], ur01[:, LANES:], ur23[:, :LANES], ur23[:, LANES:]]
        ui = [ui01[:, :LANES], ui01[:, LANES:], ui23[:, :LANES], ui23[:, LANES:]]
        vr, vn = [ur[0]], [ui[0]]
        for s2 in range(1, FFT_RADIX):
            tc = tw_ref[s2 - 1, 0, pl.ds(r0, rows), :]
            ts = tw_ref[s2 - 1, 1, pl.ds(r0, rows), :]
            vr.append(ur[s2] * tc - ui[s2] * ts)
            vn.append(ur[s2] * ts + ui[s2] * tc)
        a_r, b_r = vr[0] + vr[2], vr[0] - vr[2]
        c_r, d_r = vr[1] + vr[3], vr[1] - vr[3]
        a_n, b_n = vn[0] + vn[2], vn[0] - vn[2]
        c_n, d_n = vn[1] + vn[3], vn[1] - vn[3]
        zr = [a_r + c_r, b_r - d_n, a_r - c_r, b_r + d_n]
        zn = [a_n + c_n, b_n + d_r, a_n - c_n, b_n - d_r]
        for k2 in range(FFT_RADIX):
            z = jnp.concatenate([zr[k2], -zn[k2]], axis=1).astype(_BF)
            y_ref[0, pl.ds(pl.multiple_of(k2 * nd + r0, rows), rows), :] = _dot(z, chan).astype(_BF)
        return carry

    lax.fori_loop(0, nd // rows, body, 0)


def _fourier(f, tables):
    b, s, _ = f.shape
    dft, tw, chan = tables
    nd = s // FFT_RADIX
    rows = min(FFT_ROW_CHUNK, nd)
    blk = pl.BlockSpec((1, s, FGROUP_DIM), lambda bi, g: (bi, 0, g))
    return pl.pallas_call(
        functools.partial(_fourier_kernel, nd=nd, rows=rows),
        grid=(b, N_FGROUPS),
        in_specs=[blk, _const_spec(dft.shape), _const_spec(tw.shape), _const_spec(chan.shape)],
        out_specs=blk,
        out_shape=jax.ShapeDtypeStruct((b, s, D_FOURIER), _BF),
        compiler_params=_cparams(("parallel", "parallel")),
        name="fourier",
    )(f, dft, tw, chan)


_N_SLABS = D_RNN // LANES


def _rglru_kernel(*refs, reverse, tt, pitch, nchunks):
    if reverse:
        (rx_ref, rxp_ref, rxn_ref, cw_ref, cb_ref, wg_ref, bg_ref, lam_ref, hf_ref,
         o_ref, a_scr, b_scr, carry_scr) = refs
    else:
        (rx_ref, rxp_ref, rxn_ref, cw_ref, cb_ref, wg_ref, bg_ref, lam_ref,
         o_ref, a_scr, b_scr, carry_scr) = refs
        hf_ref = None
    h_scr = b_scr
    step = pl.program_id(0)
    chunk = (nchunks - 1 - step) if reverse else step
    nbatch = rx_ref.shape[0]

    @pl.when(step == 0)
    def _():
        carry_scr[...] = jnp.zeros_like(carry_scr)

    has_prev = chunk > 0
    has_next = chunk < nchunks - 1
    row = lax.broadcasted_iota(jnp.int32, (tt, LANES), 0)
    for sl in range(_N_SLABS):
        cs = slice(sl * LANES, (sl + 1) * LANES)
        w = [cw_ref[k:k + 1, cs] for k in range(CONV_RNN_WIDTH)]
        cb = cb_ref[:, cs]
        lam = lam_ref[:, cs]
        neg_c_sp = -RG_C * (jnp.maximum(-lam, 0.0) + jnp.log1p(jnp.exp(-jnp.abs(lam))))
        for bi in range(nbatch):
            x0 = rx_ref[bi, :, cs]
            prev8 = jnp.where(has_prev, rxp_ref[bi, :, cs], 0.0)
            next8 = jnp.where(has_next, rxn_ref[bi, :, cs], 0.0)
            xm1 = jnp.where(row == 0, prev8[7:8], pltpu.roll(x0, 1, 0))
            xm2 = jnp.where(row == 0, prev8[6:7], jnp.where(row == 1, prev8[7:8], pltpu.roll(x0, 2, 0)))
            xp1 = jnp.where(row == tt - 1, next8[0:1], pltpu.roll(x0, tt - 1, 0))
            xr = cb + w[0] * xm2 + w[1] * xm1 + w[2] * x0 + w[3] * xp1
            gates = _dot(xr.astype(_BF), wg_ref[sl]) + bg_ref[sl]
            r = _sigmoid(gates[:, :LANES])
            i = _sigmoid(gates[:, LANES:])
            log_a = r * neg_c_sp
            a = jnp.exp(log_a)
            bb = jnp.sqrt(-jnp.tanh(log_a) * (1.0 + a * a)) * i * xr
            a_scr[sl, bi * pitch:bi * pitch + tt, :] = a
            b_scr[sl, bi * pitch:bi * pitch + tt, :] = bb

    def scan_step(k, hs):
        t = (tt - 1 - k) if reverse else k
        out = []
        for sl in range(_N_SLABS):
            a_t = a_scr[sl, pl.ds(t, nbatch, stride=pitch), :]
            b_t = b_scr[sl, pl.ds(t, nbatch, stride=pitch), :]
            h = a_t * hs[sl] + b_t
            h_scr[sl, pl.ds(t, nbatch, stride=pitch), :] = h
            out.append(h)
        return tuple(out)

    h0 = tuple(carry_scr[sl] for sl in range(_N_SLABS))
    hs = lax.fori_loop(0, tt, scan_step, h0)
    for sl in range(_N_SLABS):
        carry_scr[sl] = hs[sl]
        cs = slice(sl * LANES, (sl + 1) * LANES)
        for bi in range(nbatch):
            h = h_scr[sl, bi * pitch:bi * pitch + tt, :]
            if reverse:
                h = h + hf_ref[bi, :, cs]
            o_ref[bi, :, cs] = h


def _rglru(rx, conv_w, conv_b, wg, bg, lam, h_fwd=None):
    b, s, _ = rx.shape
    assert b == SUBLANES, "the scan keeps one batch row per sublane"
    reverse = h_fwd is not None
    tt = min(RNN_TILE, s)
    nchunks = s // tt
    pitch = tt + RNN_PITCH_PAD
    hb = tt // SUBLANES
    cidx = (lambda i: nchunks - 1 - i) if reverse else (lambda i: i)
    main = pl.BlockSpec((b, tt, D_RNN), lambda i: (0, cidx(i), 0))
    prev = pl.BlockSpec((b, SUBLANES, D_RNN), lambda i: (0, jnp.maximum(cidx(i) * hb - 1, 0), 0))
    nxt = pl.BlockSpec((b, SUBLANES, D_RNN),
                       lambda i: (0, jnp.minimum((cidx(i) + 1) * hb, s // SUBLANES - 1), 0))
    in_specs = [main, prev, nxt, _const_spec(conv_w.shape), _const_spec(conv_b.shape),
                _const_spec(wg.shape), _const_spec(bg.shape), _const_spec(lam.shape)]
    args = [rx, rx, rx, conv_w, conv_b, wg, bg, lam]
    if reverse:
        in_specs.append(main)
        args.append(h_fwd)
    scr = pltpu.VMEM((_N_SLABS, b * pitch, LANES), _F32)
    return pl.pallas_call(
        functools.partial(_rglru_kernel, reverse=reverse, tt=tt, pitch=pitch, nchunks=nchunks),
        grid=(nchunks,),
        in_specs=in_specs,
        out_specs=main,
        out_shape=jax.ShapeDtypeStruct((b, s, D_RNN), _F32),
        scratch_shapes=[scr, scr, pltpu.VMEM((_N_SLABS, b, LANES), _F32)],
        compiler_params=_cparams(("arbitrary",)),
        name="rglru_bwd" if reverse else "rglru_fwd",
    )(*args)


def _merge_kernel(x_ref, att_ref, y_ref, h_ref, wg_ref, bg_ref, wa_ref, wf_ref, wr_ref, wo_ref,
                  bo_ref, g_ref, b_ref, o_ref):
    x = x_ref[...]
    xb = x.astype(_BF)
    d = D_MODEL
    ry = _dot(xb, wg_ref[:, 0:d]) + bg_ref[:, 0:d]
    hr = (h_ref[...] * _gelu(ry)).astype(_BF)
    o_rnn = _dot(hr, wr_ref[...])
    gate = lambda n: _sigmoid(_dot(xb, wg_ref[:, (n + 1) * d:(n + 2) * d]) + bg_ref[:, (n + 1) * d:(n + 2) * d])
    mixed = gate(2) * o_rnn
    mixed = mixed + gate(0) * _dot(att_ref[...], wa_ref[...])
    mixed = mixed + gate(1) * _dot(y_ref[...], wf_ref[...])
    z = DN_ALPHA * x + _dot(mixed.astype(_BF), wo_ref[...]) + bo_ref[...]
    o_ref[...] = _layer_norm(z, g_ref[...], b_ref[...])


def _merge(x2d, att, y, h, wg, bg, wa, wf, wr, wo, bo, ln_g, ln_b):
    n = x2d.shape[0]
    t = TOKEN_TILE
    row = lambda width: pl.BlockSpec((t, width), lambda i: (i, 0))
    consts = [wg, bg, wa, wf, wr, wo, bo, ln_g, ln_b]
    return pl.pallas_call(
        _merge_kernel,
        grid=(n // t,),
        in_specs=[row(D_MODEL), row(D_ATT), row(D_FOURIER), row(D_RNN)] + [_const_spec(c.shape) for c in consts],
        out_specs=row(D_MODEL),
        out_shape=jax.ShapeDtypeStruct((n, D_MODEL), _F32),
        compiler_params=_cparams(("parallel",)),
        name="merge",
    )(x2d, att, y, h, *consts)


def _ffn_kernel(x_ref, xp_ref, xn_ref, wu_ref, cw_ref, cb_ref, wd_ref, g_ref, b_ref, o_ref, *, tiles_per_seq):
    i = pl.program_id(0)
    pos = i % tiles_per_seq
    t = x_ref.shape[0]
    x = x_ref[...]
    xb = x.astype(_BF)
    xpb = xp_ref[...].astype(_BF)
    xnb = xn_ref[...].astype(_BF)
    keep_prev = pos > 0
    keep_next = pos < tiles_per_seq - 1
    row = lax.broadcasted_iota(jnp.int32, (t, FFN_CHUNK), 0)
    acc = jnp.zeros((t, D_MODEL), _F32)
    for c in range(D_FF // FFN_CHUNK):
        halves = []
        for base in (0, D_FF):
            cs = slice(base + c * FFN_CHUNK, base + (c + 1) * FFN_CHUNK)
            wu = wu_ref[:, cs]
            hu = _dot(xb, wu)
            hp = jnp.where(keep_prev, _dot(xpb, wu)[SUBLANES - 1:SUBLANES], 0.0)
            hn = jnp.where(keep_next, _dot(xnb, wu)[0:1], 0.0)
            hm1 = jnp.where(row == 0, hp, pltpu.roll(hu, 1, 0))
            hp1 = jnp.where(row == t - 1, hn, pltpu.roll(hu, t - 1, 0))
            halves.append(cb_ref[:, cs] + cw_ref[0:1, cs] * hm1 + cw_ref[1:2, cs] * hu + cw_ref[2:3, cs] * hp1)
        act = (_gelu(halves[0]) * halves[1]).astype(_BF)
        acc = acc + _dot(act, wd_ref[c * FFN_CHUNK:(c + 1) * FFN_CHUNK, :])
    o_ref[...] = _layer_norm(DN_ALPHA * x + acc, g_ref[...], b_ref[...])


def _ffn(x2d, seq_len, wu, cw, cb, wd, ln_g, ln_b):
    n = x2d.shape[0]
    t = min(TOKEN_TILE, seq_len)
    tiles_per_seq = seq_len // t
    hb = t // SUBLANES
    nblk8 = n // SUBLANES
    row = pl.BlockSpec((t, D_MODEL), lambda i: (i, 0))
    prev = pl.BlockSpec((SUBLANES, D_MODEL), lambda i: (jnp.maximum(i * hb - 1, 0), 0))
    nxt = pl.BlockSpec((SUBLANES, D_MODEL), lambda i: (jnp.minimum((i + 1) * hb, nblk8 - 1), 0))
    consts = [wu, cw, cb, wd, ln_g, ln_b]
    return pl.pallas_call(
        functools.partial(_ffn_kernel, tiles_per_seq=tiles_per_seq),
        grid=(n // t,),
        in_specs=[row, prev, nxt] + [_const_spec(c.shape) for c in consts],
        out_specs=row,
        out_shape=jax.ShapeDtypeStruct((n, D_MODEL), _F32),
        compiler_params=_cparams(("parallel",)),
        name="ffn",
    )(x2d, x2d, x2d, *consts)


def _dup_heads(w):
    h0, h1 = w[..., :HEAD_DIM], w[..., HEAD_DIM:]
    return jnp.concatenate([h0, h0, h1, h1], axis=-1)


def _rg_gate_weights(w_r, b_r, w_i, b_i):
    def slabs(w):
        w = w.reshape(_N_SLABS, 2, RG_BLOCK_DIM, RG_BLOCK_DIM)
        z = jnp.zeros_like(w[:, 0])
        top = jnp.concatenate([w[:, 0], z], axis=2)
        bot = jnp.concatenate([z, w[:, 1]], axis=2)
        return jnp.concatenate([top, bot], axis=1)
    wg = jnp.concatenate([slabs(w_r), slabs(w_i)], axis=2).astype(_BF)
    bg = jnp.concatenate([b_r.reshape(_N_SLABS, 1, LANES), b_i.reshape(_N_SLABS, 1, LANES)], axis=2)
    return wg, bg.astype(_F32)


def _layer_params(l, w_in, b_in, attn_sink, w_att_o, w_four_o, conv_rnn_w, conv_rnn_b, w_rg_r, b_rg_r,
                  w_rg_i, b_rg_i, rg_lambda, w_rnn_o, w_out, b_out, ln1_g, ln1_b, w_ffn_up, conv_ffn_w,
                  conv_ffn_b, w_ffn_down, ln2_g, ln2_b):
    wi, bi = w_in[l], b_in[l]
    row = lambda v: v.reshape(1, -1).astype(_F32)
    w1 = jnp.concatenate([wi[:, :Q_END], _dup_heads(wi[:, Q_END:K_END]), _dup_heads(wi[:, K_END:V_END]),
                          wi[:, V_END:RX_END]], axis=1).astype(_BF)
    b1 = row(jnp.concatenate([bi[:Q_END], _dup_heads(bi[Q_END:K_END]), _dup_heads(bi[K_END:V_END]),
                              bi[V_END:RX_END]]))
    sink = jnp.repeat(attn_sink[l].astype(_F32), BLOCK).reshape(N_KV_HEADS, _GROUP * BLOCK, 1)
    rg = [_rg_gate_weights(w_rg_r[l, d], b_rg_r[l, d], w_rg_i[l, d], b_rg_i[l, d]) for d in range(2)]
    return dict(
        w1=w1, b1=b1, sink=sink,
        conv_w=conv_rnn_w[l].astype(_F32), conv_b=row(conv_rnn_b[l]),
        rg=rg, lam=[row(rg_lambda[l, d]) for d in range(2)],
        wg=wi[:, RX_END:].astype(_BF), bg=row(bi[RX_END:]),
        wa=w_att_o[l].astype(_BF), wf=w_four_o[l].astype(_BF), wr=w_rnn_o[l].astype(_BF),
        wo=w_out[l].astype(_BF), bo=row(b_out[l]), ln1_g=row(ln1_g[l]), ln1_b=row(ln1_b[l]),
        wu=w_ffn_up[l].astype(_BF), cw=conv_ffn_w[l].astype(_F32), cb=row(conv_ffn_b[l]),
        wd=w_ffn_down[l].astype(_BF), ln2_g=row(ln2_g[l]), ln2_b=row(ln2_b[l]),
    )


def _encoder_layer(x, p, bias_g, fft_tables):
    b, s, d = x.shape
    x2d = x.reshape(b * s, d)
    q, k2, v2, f, rx = _in_proj(x2d, p["w1"], p["b1"])
    att = _attention(q.reshape(b, s, -1), k2.reshape(b, s, -1), v2.reshape(b, s, -1), bias_g, p["sink"])
    y = _fourier(f.reshape(b, s, -1), fft_tables)
    rx3 = rx.reshape(b, s, -1)
    h_f = _rglru(rx3, p["conv_w"], p["conv_b"], p["rg"][0][0], p["rg"][0][1], p["lam"][0])
    h = _rglru(rx3, p["conv_w"], p["conv_b"], p["rg"][1][0], p["rg"][1][1], p["lam"][1], h_fwd=h_f)
    x1 = _merge(x2d, att.reshape(b * s, -1), y.reshape(b * s, -1), h.reshape(b * s, -1),
                p["wg"], p["bg"], p["wa"], p["wf"], p["wr"], p["wo"], p["bo"], p["ln1_g"], p["ln1_b"])
    x2 = _ffn(x1, s, p["wu"], p["cw"], p["cb"], p["wd"], p["ln2_g"], p["ln2_b"])
    return x2.reshape(b, s, d)


def kernel(x_prompt, x_sample, rel_bias, w_in, b_in, attn_sink, w_att_o, w_four_o, conv_rnn_w, conv_rnn_b,
           w_rg_r, b_rg_r, w_rg_i, b_rg_i, rg_lambda, w_rnn_o, w_out, b_out, ln1_g, ln1_b, w_ffn_up,
           conv_ffn_w, conv_ffn_b, w_ffn_down, ln2_g, ln2_b):
    weights = (w_in, b_in, attn_sink, w_att_o, w_four_o, conv_rnn_w, conv_rnn_b, w_rg_r, b_rg_r, w_rg_i,
               b_rg_i, rg_lambda, w_rnn_o, w_out, b_out, ln1_g, ln1_b, w_ffn_up, conv_ffn_w, conv_ffn_b,
               w_ffn_down, ln2_g, ln2_b)
    depth = w_in.shape[0]
    bias = _bias_table(rel_bias)
    bias_g = bias.reshape(N_KV_HEADS, _GROUP * BLOCK, 3 * BLOCK)
    params = [_layer_params(l, *weights) for l in range(depth)]
    outs = []
    for x in (x_prompt, x_sample):
        tables = _fft_tables(x.shape[1])
        y = x
        for l in range(depth):
            y = _encoder_layer(y, params[l], bias_g, tables)
        outs.append(y)
    return tuple(outs)
```
